```python
import math
import jax, jax.numpy as jnp
from jax import lax
import numpy as np

D_MODEL = 1024
BATCH = 2
SEQ = 8192
DEPTH = 4
DEC_BATCH = 32
DEC_SEQ = 2048
PAST_LEN = 128

N_MIXERS = 3
D_FF = 2816
GM_E = 3 * D_MODEL
GM_CHUNK = 128
GM_GROUP_W = 128
GM_GROUPS = GM_E // GM_GROUP_W
ML_E = 2 * D_MODEL
ML_HEADS = 4
ML_DH = ML_E // ML_HEADS
ML_CONV_K = 5
ML_QKV_BLOCK = 4
ML_CHUNK = 128
MLA_HEADS = 8
MLA_NOPE = 128
MLA_ROPE = 64
MLA_V = 128
MLA_Q_LORA = 384
MLA_KV_LORA = 256
MLA_QBLOCK = 128
ROPE_THETA = 10000.0
LN_EPS = 1e-5
NORM_EPS = 1e-6
DEEPNORM_ALPHA = (2 * DEPTH) ** 0.25
DEEPNORM_BETA = (8 * DEPTH) ** -0.25

kernel_name = 'hybrid_bidir_gmlp_mlstm_mla_encoder'


def _layernorm(x, g, b, eps=LN_EPS):
    xf = x.astype(jnp.float32)
    mu = jnp.mean(xf, -1, keepdims=True)
    var = jnp.mean(jnp.square(xf - mu), -1, keepdims=True)
    return ((xf - mu) * lax.rsqrt(var + eps)).astype(x.dtype) * g + b


def _rmsnorm(x, g, eps=NORM_EPS):
    xf = x.astype(jnp.float32)
    return (xf * lax.rsqrt(jnp.mean(jnp.square(xf), -1, keepdims=True) + eps)).astype(x.dtype) * g


def _swiglu(h, w_in, w_out):
    gate, up = jnp.split(h @ w_in, 2, axis=-1)
    return (jax.nn.silu(gate) * up) @ w_out


def _gmlp_mixer(h, w_in, b_in, ln_g, ln_b, w_s, b_s, w_out):
    B, S, _ = h.shape
    z = jax.nn.gelu(h @ w_in + b_in, approximate=False)
    u, v = jnp.split(z, 2, axis=-1)
    v = _layernorm(v, ln_g, ln_b)
    v = v.reshape(B, S // GM_CHUNK, GM_CHUNK, GM_GROUPS, GM_GROUP_W)
    v = jnp.einsum('gpq,bnqgc->bnpgc', w_s, v) + b_s.T[None, None, :, :, None]
    return (u * v.reshape(B, S, GM_E)) @ w_out


def _headwise(x, w):
    B, S, E = x.shape
    xb = x.reshape(B, S, E // ML_QKV_BLOCK, ML_QKV_BLOCK)
    return jnp.einsum('bsnj,nji->bsni', xb, w).reshape(B, S, E)


def _mlstm_scan(q, k, v, i_pre, f_pre):
    B, H, S, DH = q.shape
    nc = S // ML_CHUNK

    def chunks(t):
        return jnp.moveaxis(t.reshape((B, H, nc, ML_CHUNK) + t.shape[3:]), 2, 0)

    k = k * DH ** -0.5
    log_f = jax.nn.log_sigmoid(f_pre)
    xs = (chunks(q), chunks(k), chunks(v), chunks(i_pre), chunks(log_f))
    tril = jnp.tril(jnp.ones((ML_CHUNK, ML_CHUNK), dtype=bool))

    def step(carry, inp):
        C, n, m = carry
        qc, kc, vc, ic, lfc = inp
        a = jnp.cumsum(lfc, axis=-1)
        d = jnp.where(tril, a[..., :, None] - a[..., None, :] + ic[..., None, :], -jnp.inf)
        inter = a + m[..., None]
        m_j = jnp.maximum(inter, jnp.max(d, -1))
        w_inter = jnp.exp(inter - m_j)
        s = jnp.einsum('bhld,bhrd->bhlr', qc, kc) * jnp.exp(d - m_j[..., None])
        num = w_inter[..., None] * jnp.einsum('bhld,bhde->bhle', qc, C) + jnp.einsum('bhlr,bhre->bhle', s, vc)
        den = w_inter * jnp.einsum('bhld,bhd->bhl', qc, n) + jnp.sum(s, -1)
        h = num / jnp.maximum(jnp.abs(den), jnp.exp(-m_j))[..., None]
        a_last = a[..., -1]
        dec = a_last[..., None] - a + ic
        m_new = jnp.maximum(a_last + m, jnp.max(dec, -1))
        w_k = jnp.exp(dec - m_new[..., None])
        keep = jnp.exp(a_last + m - m_new)
        C = keep[..., None, None] * C + jnp.einsum('bhl,bhld,bhle->bhde', w_k, kc, vc)
        n = keep[..., None] * n + jnp.einsum('bhl,bhld->bhd', w_k, kc)
        return (C, n, m_new), h

    init = (jnp.zeros((B, H, DH, DH), jnp.float32),
            jnp.zeros((B, H, DH), jnp.float32),
            jnp.full((B, H), -jnp.inf, jnp.float32))
    _, hs = lax.scan(step, init, xs)
    return jnp.moveaxis(hs, 0, 2).reshape(B, H, S, DH)


def _mlstm_mixer(h, w_up, conv_w, conv_b, w_q, w_k, w_v, w_gates, b_gates, gn_g, skip, w_down):
    B, S, _ = h.shape
    xm, z = jnp.split(h @ w_up, 2, axis=-1)
    xc = lax.conv_general_dilated(xm, conv_w[:, None, :], (1,), [(ML_CONV_K // 2, ML_CONV_K // 2)],
                                  dimension_numbers=('NWC', 'WIO', 'NWC'),
                                  feature_group_count=ML_E) + conv_b
    xc = jax.nn.silu(xc)
    q = _headwise(xc, w_q)
    k = _headwise(xc, w_k)
    v = _headwise(xm, w_v)
    gates = (jnp.concatenate([q, k, v], -1) @ w_gates + b_gates).astype(jnp.float32)
    gates = jnp.moveaxis(gates.reshape(B, S, 4, ML_HEADS), 1, -1)

    def heads(t):
        return t.reshape(B, S, ML_HEADS, ML_DH).transpose(0, 2, 1, 3).astype(jnp.float32)

    qh, kh, vh = heads(q), heads(k), heads(v)
    flip = lambda t: jnp.flip(t, axis=2)
    h_fwd = _mlstm_scan(qh, kh, vh, gates[:, 0], gates[:, 1])
    h_bwd = flip(_mlstm_scan(flip(qh), flip(kh), flip(vh), flip(gates[:, 2]), flip(gates[:, 3])))
    hs = h_fwd + h_bwd
    mu = jnp.mean(hs, -1, keepdims=True)
    var = jnp.mean(jnp.square(hs - mu), -1, keepdims=True)
    hn = ((hs - mu) * lax.rsqrt(var + NORM_EPS)).transpose(0, 2, 1, 3).reshape(B, S, ML_E)
    hn = hn.astype(h.dtype) * gn_g
    return ((hn + skip * xc) * jax.nn.silu(z)) @ w_down


def _rope_tables(S, dtype):
    inv = ROPE_THETA ** (-jnp.arange(0, MLA_ROPE, 2, dtype=jnp.float32) / MLA_ROPE)
    ang = jnp.arange(S, dtype=jnp.float32)[:, None] * inv[None, :]
    return jnp.cos(ang).astype(dtype), jnp.sin(ang).astype(dtype)


def _rope(x, cos, sin):
    x1, x2 = jnp.split(x, 2, axis=-1)
    return jnp.concatenate([x1 * cos - x2 * sin, x2 * cos + x1 * sin], -1)


def _mla_mixer(h, w_down, q_norm, kv_norm, w_uq, w_ukv, w_o):
    B, S, _ = h.shape
    cq, ckv, k_rope = jnp.split(h @ w_down, [MLA_Q_LORA, MLA_Q_LORA + MLA_KV_LORA], axis=-1)
    cq = _rmsnorm(cq, q_norm)
    ckv = _rmsnorm(ckv, kv_norm)
    q = (cq @ w_uq).reshape(B, S, MLA_HEADS, MLA_NOPE + MLA_ROPE)
    q_nope, q_rope = jnp.split(q, [MLA_NOPE], axis=-1)
    kv = (ckv @ w_ukv).reshape(B, S, MLA_HEADS, MLA_NOPE + MLA_V)
    k_nope, v = jnp.split(kv, [MLA_NOPE], axis=-1)
    cos, sin = _rope_tables(S, h.dtype)
    q_rope = _rope(q_rope, cos[:, None], sin[:, None])
    k_rope = _rope(k_rope, cos, sin)
    scale = (MLA_NOPE + MLA_ROPE) ** -0.5
    nq = S // MLA_QBLOCK

    def qblocks(t):
        return jnp.moveaxis(t.reshape(B, nq, MLA_QBLOCK, MLA_HEADS, t.shape[-1]), 1, 0)

    def attend(blk):
        qn, qr = blk
        s = jnp.einsum('bqhd,bkhd->bhqk', qn, k_nope) + jnp.einsum('bqhr,bkr->bhqk', qr, k_rope)
        p = jax.nn.softmax(s.astype(jnp.float32) * scale, axis=-1).astype(v.dtype)
        return jnp.einsum('bhqk,bkhd->bqhd', p, v)

    o = lax.map(attend, (qblocks(q_nope), qblocks(q_rope)))
    o = jnp.moveaxis(o, 0, 1).reshape(B, S, MLA_HEADS * MLA_V)
    return o @ w_o


def _trunk(x, cond, p):
    B = x.shape[0]
    cond_act = jax.nn.silu(cond)
    counts = [0, 0, 0]
    for layer in range(DEPTH):
        mod = (cond_act @ p['ada_w'][layer] + p['ada_b'][layer]).reshape(B, 3, 3, 1, D_MODEL)

        def sublayer(x, j, fn):
            shift, scale, gate = mod[:, j, 0], mod[:, j, 1], mod[:, j, 2]
            y = fn(x * (1 + scale) + shift)
            return _layernorm(DEEPNORM_ALPHA * x + gate * y, p['ln_g'][layer, j], p['ln_b'][layer, j])

        x = sublayer(x, 0, lambda h: 0.5 * _swiglu(h, p['ffn_w_in'][layer, 0], p['ffn_w_out'][layer, 0]))
        kind = layer % N_MIXERS
        i = counts[kind]
        counts[kind] += 1
        if kind == 0:
            fn = lambda h: _gmlp_mixer(h, p['gm_w_in'][i], p['gm_b_in'][i], p['gm_ln_g'][i], p['gm_ln_b'][i],
                                       p['gm_w_s'][i], p['gm_b_s'][i], p['gm_w_out'][i])
        elif kind == 1:
            fn = lambda h: _mlstm_mixer(h, p['ml_w_up'][i], p['ml_conv_w'][i], p['ml_conv_b'][i],
                                        p['ml_w_q'][i], p['ml_w_k'][i], p['ml_w_v'][i],
                                        p['ml_w_gates'][i], p['ml_b_gates'][i], p['ml_gn_g'][i],
                                        p['ml_skip'][i], p['ml_w_down'][i])
        else:
            fn = lambda h: _mla_mixer(h, p['mla_w_down'][i], p['mla_q_norm'][i], p['mla_kv_norm'][i],
                                      p['mla_w_uq'][i], p['mla_w_ukv'][i], p['mla_w_o'][i])
        x = sublayer(x, 1, fn)
        x = sublayer(x, 2, lambda h: 0.5 * _swiglu(h, p['ffn_w_in'][layer, 1], p['ffn_w_out'][layer, 1]))
    return x


def setup_inputs(seed: int = 0) -> dict:
    key = jax.random.key(seed)
    ks = iter(jax.random.split(key, 64))
    f32 = jnp.float32

    def nrm(shape, std):
        return jax.random.normal(next(ks), shape, f32) * std

    D = D_MODEL
    beta = DEEPNORM_BETA
    n_a = sum(1 for l in range(DEPTH) if l % N_MIXERS == 0)
    n_b = sum(1 for l in range(DEPTH) if l % N_MIXERS == 1)
    n_c = sum(1 for l in range(DEPTH) if l % N_MIXERS == 2)
    f_bias = jnp.linspace(3.0, 6.0, ML_HEADS, dtype=f32)
    inp = {}
    inp['x_prompt'] = nrm((BATCH, SEQ, D), 1.0)
    inp['x_sample'] = nrm((DEC_BATCH, DEC_SEQ, D), 1.0)
    inp['c_prompt'] = nrm((BATCH, D), 1.0)
    inp['c_sample'] = nrm((DEC_BATCH, D), 1.0)
    inp['ada_w'] = nrm((DEPTH, D, 9 * D), 0.5 * D ** -0.5)
    inp['ada_b'] = nrm((DEPTH, 9 * D), 0.02)
    inp['ln_g'] = 1.0 + nrm((DEPTH, 3, D), 0.02)
    inp['ln_b'] = nrm((DEPTH, 3, D), 0.02)
    inp['ffn_w_in'] = nrm((DEPTH, 2, D, 2 * D_FF), D ** -0.5)
    inp['ffn_w_out'] = nrm((DEPTH, 2, D_FF, D), beta * D_FF ** -0.5)
    inp['gm_w_in'] = nrm((n_a, D, 2 * GM_E), D ** -0.5)
    inp['gm_b_in'] = nrm((n_a, 2 * GM_E), 0.02)
    inp['gm_ln_g'] = 1.0 + nrm((n_a, GM_E), 0.02)
    inp['gm_ln_b'] = nrm((n_a, GM_E), 0.02)
    inp['gm_w_s'] = nrm((n_a, GM_GROUPS, GM_CHUNK, GM_CHUNK), GM_CHUNK ** -0.5)
    inp['gm_b_s'] = 1.0 + nrm((n_a, GM_GROUPS, GM_CHUNK), 0.02)
    inp['gm_w_out'] = nrm((n_a, GM_E, D), beta * GM_E ** -0.5)
    inp['ml_w_up'] = nrm((n_b, D, 2 * ML_E), D ** -0.5)
    inp['ml_conv_w'] = nrm((n_b, ML_CONV_K, ML_E), ML_CONV_K ** -0.5)
    inp['ml_conv_b'] = nrm((n_b, ML_E), 0.02)
    nblk = ML_E // ML_QKV_BLOCK
    inp['ml_w_q'] = nrm((n_b, nblk, ML_QKV_BLOCK, ML_QKV_BLOCK), ML_QKV_BLOCK ** -0.5)
    inp['ml_w_k'] = nrm((n_b, nblk, ML_QKV_BLOCK, ML_QKV_BLOCK), ML_QKV_BLOCK ** -0.5)
    inp['ml_w_v'] = nrm((n_b, nblk, ML_QKV_BLOCK, ML_QKV_BLOCK), ML_QKV_BLOCK ** -0.5)
    inp['ml_w_gates'] = nrm((n_b, 3 * ML_E, 4 * ML_HEADS), 0.01)
    i_fwd = nrm((n_b, ML_HEADS), 0.1)
    f_fwd = f_bias + nrm((n_b, ML_HEADS), 0.02)
    i_bwd = nrm((n_b, ML_HEADS), 0.1)
    f_bwd = f_bias + nrm((n_b, ML_HEADS), 0.02)
    inp['ml_b_gates'] = jnp.concatenate([i_fwd, f_fwd, i_bwd, f_bwd], axis=-1)
    inp['ml_gn_g'] = 1.0 + nrm((n_b, ML_E), 0.02)
    inp['ml_skip'] = 1.0 + nrm((n_b, ML_E), 0.02)
    inp['ml_w_down'] = nrm((n_b, ML_E, D), beta * ML_E ** -0.5)
    inp['mla_w_down'] = nrm((n_c, D, MLA_Q_LORA + MLA_KV_LORA + MLA_ROPE), D ** -0.5)
    inp['mla_q_norm'] = 1.0 + nrm((n_c, MLA_Q_LORA), 0.02)
    inp['mla_kv_norm'] = 1.0 + nrm((n_c, MLA_KV_LORA), 0.02)
    inp['mla_w_uq'] = nrm((n_c, MLA_Q_LORA, MLA_HEADS * (MLA_NOPE + MLA_ROPE)), MLA_Q_LORA ** -0.5)
    inp['mla_w_ukv'] = nrm((n_c, MLA_KV_LORA, MLA_HEADS * (MLA_NOPE + MLA_V)), MLA_KV_LORA ** -0.5)
    inp['mla_w_o'] = nrm((n_c, MLA_HEADS * MLA_V, D), beta * (MLA_HEADS * MLA_V) ** -0.5)
    return inp


def reference(x_prompt, x_sample, c_prompt, c_sample, ada_w, ada_b, ln_g, ln_b, ffn_w_in, ffn_w_out,
              gm_w_in, gm_b_in, gm_ln_g, gm_ln_b, gm_w_s, gm_b_s, gm_w_out,
              ml_w_up, ml_conv_w, ml_conv_b, ml_w_q, ml_w_k, ml_w_v, ml_w_gates, ml_b_gates,
              ml_gn_g, ml_skip, ml_w_down,
              mla_w_down, mla_q_norm, mla_kv_norm, mla_w_uq, mla_w_ukv, mla_w_o):
    p = dict(ada_w=ada_w, ada_b=ada_b, ln_g=ln_g, ln_b=ln_b, ffn_w_in=ffn_w_in, ffn_w_out=ffn_w_out,
             gm_w_in=gm_w_in, gm_b_in=gm_b_in, gm_ln_g=gm_ln_g, gm_ln_b=gm_ln_b, gm_w_s=gm_w_s,
             gm_b_s=gm_b_s, gm_w_out=gm_w_out,
             ml_w_up=ml_w_up, ml_conv_w=ml_conv_w, ml_conv_b=ml_conv_b, ml_w_q=ml_w_q, ml_w_k=ml_w_k,
             ml_w_v=ml_w_v, ml_w_gates=ml_w_gates, ml_b_gates=ml_b_gates, ml_gn_g=ml_gn_g,
             ml_skip=ml_skip, ml_w_down=ml_w_down,
             mla_w_down=mla_w_down, mla_q_norm=mla_q_norm, mla_kv_norm=mla_kv_norm,
             mla_w_uq=mla_w_uq, mla_w_ukv=mla_w_ukv, mla_w_o=mla_w_o)
    y_prompt = _trunk(x_prompt, c_prompt, p)
    y_sample = _trunk(x_sample, c_sample, p)
    return (y_prompt, y_sample)
```

```python
import functools
import math

import jax
import jax.numpy as jnp
import numpy as np
from jax import lax
from jax.experimental import pallas as pl
from jax.experimental.pallas import tpu as pltpu

F32 = jnp.float32
BF16 = jnp.bfloat16

D_MODEL = 1024
DEPTH = 4
N_MIXERS = 3
D_FF = 2816
GM_E = 3 * D_MODEL
GM_CHUNK = 128
GM_GROUP_W = 128
GM_GROUPS = GM_E // GM_GROUP_W
ML_E = 2 * D_MODEL
ML_HEADS = 4
ML_DH = ML_E // ML_HEADS
ML_CONV_K = 5
ML_QKV_BLOCK = 4
ML_CHUNK = 128
MLA_HEADS = 8
MLA_NOPE = 128
MLA_ROPE = 64
MLA_V = 128
MLA_Q_LORA = 384
MLA_KV_LORA = 256
ROPE_THETA = 10000.0
LN_EPS = 1e-5
NORM_EPS = 1e-6
DEEPNORM_ALPHA = (2 * DEPTH) ** 0.25

LANES = 128
SUBLANES = 8
MXU_DIM = 256
VMEM_LIMIT_BYTES = 56 * 1024 * 1024

FFN_TM = 512
FFN_CH = D_FF // 2
GMLP_TM = 256
ML_PRE_TM = 256
ML_POST_TM = 512
MLA_PRE_TM = 512
MLA_POST_TM = 512
ATTN_TQ = 512
ATTN_KC = 2048
ADA_TN = 1152
ML_QKV_BD = MXU_DIM
MLA_QK_W = 2 * LANES


def _params(n_grid_axes):
    return pltpu.CompilerParams(
        dimension_semantics=("arbitrary",) * n_grid_axes,
        vmem_limit_bytes=VMEM_LIMIT_BYTES,
    )


def _resident(shape):
    nd = len(shape)
    return pl.BlockSpec(shape, lambda *_: (0,) * nd, pipeline_mode=pl.Buffered(1))


def _modulated(x, mod_ref, j):
    shift = mod_ref[3 * j:3 * j + 1, :]
    scale = mod_ref[3 * j + 1:3 * j + 2, :]
    return (x * (1.0 + scale) + shift).astype(BF16)


def _close(x, y, mod_ref, j, g_ref, b_ref):
    gate = mod_ref[3 * j + 2:3 * j + 3, :]
    v = DEEPNORM_ALPHA * x + gate * y
    mu = jnp.mean(v, axis=-1, keepdims=True)
    c = v - mu
    var = jnp.mean(c * c, axis=-1, keepdims=True)
    return c * lax.rsqrt(var + LN_EPS) * g_ref[...] + b_ref[...]


def _silu(x):
    return x * jax.nn.sigmoid(x)


def _ada_kernel(c_ref, w_ref, b_ref, o_ref):
    a = _silu(c_ref[...]).astype(BF16)
    o_ref[...] = jnp.dot(a, w_ref[...].astype(BF16), preferred_element_type=F32) + b_ref[...]


def _ada_table(cond, ada_w, ada_b):
    ns = cond.shape[0]
    nd = ada_w.shape[-1]
    return pl.pallas_call(
        _ada_kernel,
        grid=(DEPTH, nd // ADA_TN),
        in_specs=[
            pl.BlockSpec((ns, D_MODEL), lambda l, n: (0, 0)),
            pl.BlockSpec((None, D_MODEL, ADA_TN), lambda l, n: (l, 0, n)),
            pl.BlockSpec((None, 1, ADA_TN), lambda l, n: (l, 0, n)),
        ],
        out_specs=pl.BlockSpec((None, ns, ADA_TN), lambda l, n: (l, 0, n)),
        out_shape=jax.ShapeDtypeStruct((DEPTH, ns, nd), F32),
        compiler_params=_params(2),
        name="ada_table",
    )(cond, ada_w, ada_b.reshape(DEPTH, 1, nd))


def _ffn_kernel(x_ref, mod_ref, win_ref, wout_ref, g_ref, b_ref, o_ref, *, j):
    x = x_ref[...]
    h = _modulated(x, mod_ref, j)
    n_chunks, _, two_ch = win_ref.shape
    ch = two_ch // 2
    acc = jnp.zeros(x.shape, F32)
    for c in range(n_chunks):
        gu = jnp.dot(h, win_ref[c], preferred_element_type=F32)
        a = (_silu(gu[:, :ch]) * gu[:, ch:]).astype(BF16)
        acc = acc + jnp.dot(a, wout_ref[c], preferred_element_type=F32)
    o_ref[...] = _close(x, 0.5 * acc, mod_ref, j, g_ref, b_ref)


def _ffn_sublayer(x, mod_l, j, w_in_r, w_out_r, g, b, seg):
    t = x.shape[0]
    tm = min(FFN_TM, seg)
    return pl.pallas_call(
        functools.partial(_ffn_kernel, j=j),
        grid=(t // tm,),
        in_specs=[
            pl.BlockSpec((tm, D_MODEL), lambda i: (i, 0)),
            pl.BlockSpec((None, 9, D_MODEL), lambda i: (i * tm // seg, 0, 0)),
            _resident(w_in_r.shape),
            _resident(w_out_r.shape),
            _resident((1, D_MODEL)),
            _resident((1, D_MODEL)),
        ],
        out_specs=pl.BlockSpec((tm, D_MODEL), lambda i: (i, 0)),
        out_shape=jax.ShapeDtypeStruct((t, D_MODEL), F32),
        compiler_params=_params(1),
        name="ffn_sublayer",
    )(x, mod_l, w_in_r, w_out_r, g, b)


def _prep_ffn(w_in, w_out):
    n_chunks = D_FF // FFN_CH
    w_in_r = w_in.reshape(D_MODEL, 2, n_chunks, FFN_CH).transpose(2, 0, 1, 3)
    w_in_r = w_in_r.reshape(n_chunks, D_MODEL, 2 * FFN_CH).astype(BF16)
    w_out_r = w_out.reshape(n_chunks, FFN_CH, D_MODEL).astype(BF16)
    return w_in_r, w_out_r


def _gelu(x):
    return 0.5 * x * (1.0 + lax.erf(x * (2.0 ** -0.5)))


def _gmlp_kernel(x_ref, mod_ref, win_ref, bin_ref, lng_ref, lnb_ref, ws_ref, bs_ref,
                 wout_ref, g_ref, b_ref, o_ref, vb_ref, mix_ref):
    x = x_ref[...]
    tm = x.shape[0]
    h = _modulated(x, mod_ref, 1)
    v = _gelu(jnp.dot(h, win_ref[:, GM_E:], preferred_element_type=F32) + bin_ref[:, GM_E:])
    mu = jnp.mean(v, axis=-1, keepdims=True)
    c = v - mu
    var = jnp.mean(c * c, axis=-1, keepdims=True)
    vb_ref[...] = (c * lax.rsqrt(var + LN_EPS) * lng_ref[...] + lnb_ref[...]).astype(BF16)
    for n in range(tm // GM_CHUNK):
        rows = slice(n * GM_CHUNK, (n + 1) * GM_CHUNK)
        for grp in range(GM_GROUPS):
            cols = slice(grp * GM_GROUP_W, (grp + 1) * GM_GROUP_W)
            mix_ref[rows, cols] = (
                jnp.dot(ws_ref[grp], vb_ref[rows, cols], preferred_element_type=F32)
                + bs_ref[:, cols])
    u = _gelu(jnp.dot(h, win_ref[:, :GM_E], preferred_element_type=F32) + bin_ref[:, :GM_E])
    y = jnp.dot((u * mix_ref[...]).astype(BF16), wout_ref[...], preferred_element_type=F32)
    o_ref[...] = _close(x, y, mod_ref, 1, g_ref, b_ref)


def _gmlp_sublayer(x, mod_l, p, g, b, seg):
    t = x.shape[0]
    tm = min(GMLP_TM, seg)
    w_in = p["w_in"].astype(BF16)
    b_in = p["b_in"].reshape(1, 2 * GM_E)
    ln_g = p["ln_g"].reshape(1, GM_E)
    ln_b = p["ln_b"].reshape(1, GM_E)
    w_s = p["w_s"].astype(BF16)
    bs_full = jnp.repeat(p["b_s"].T, GM_GROUP_W, axis=1)
    w_out = p["w_out"].astype(BF16)
    return pl.pallas_call(
        _gmlp_kernel,
        grid=(t // tm,),
        in_specs=[
            pl.BlockSpec((tm, D_MODEL), lambda i: (i, 0)),
            pl.BlockSpec((None, 9, D_MODEL), lambda i: (i * tm // seg, 0, 0)),
            _resident(w_in.shape),
            _resident(b_in.shape),
            _resident(ln_g.shape),
            _resident(ln_b.shape),
            _resident(w_s.shape),
            _resident(bs_full.shape),
            _resident(w_out.shape),
            _resident((1, D_MODEL)),
            _resident((1, D_MODEL)),
        ],
        out_specs=pl.BlockSpec((tm, D_MODEL), lambda i: (i, 0)),
        out_shape=jax.ShapeDtypeStruct((t, D_MODEL), F32),
        scratch_shapes=[pltpu.VMEM((tm, GM_E), BF16), pltpu.VMEM((tm, GM_E), F32)],
        compiler_params=_params(1),
        name="gmlp_sublayer",
    )(x, mod_l, w_in, b_in, ln_g, ln_b, w_s, bs_full, w_out, g, b)


def _seq_pos(row0, tp, sp, ss):
    in_p = row0 < tp
    pos = jnp.where(in_p, lax.rem(row0, sp), lax.rem(row0 - tp, ss))
    return pos, jnp.where(in_p, sp, ss)


def _split3_bf16(x):
    hi = x.astype(BF16)
    r1 = x - hi.astype(F32)
    mid = r1.astype(BF16)
    lo = (r1 - mid.astype(F32)).astype(BF16)
    return hi, mid, lo


def _ml_pre_kernel(x_ref, xp_ref, xn_ref, mod_ref, wup_ref, cw_ref, cb_ref, wq_ref, wk_ref,
                   wv_ref, wg_ref, bg_ref, q_ref, k_ref, v_ref, xc_ref, z_ref, gt_ref,
                   ext_ref, *, tp, sp, ss):
    tm = x_ref.shape[0]
    halo = SUBLANES
    pad = ML_CONV_K // 2
    pos, slen = _seq_pos(pl.program_id(0) * tm, tp, sp, ss)
    keep_prev = jnp.where(pos == 0, 0.0, 1.0)
    keep_next = jnp.where(pos + tm == slen, 0.0, 1.0)

    h = _modulated(x_ref[...], mod_ref, 1)
    up = jnp.dot(h, wup_ref[...], preferred_element_type=F32)
    xm = up[:, :ML_E]
    z_ref[...] = up[:, ML_E:].astype(BF16)
    hp = _modulated(xp_ref[...], mod_ref, 1)
    hn = _modulated(xn_ref[...], mod_ref, 1)
    ext_ref[0:halo, :] = keep_prev * jnp.dot(hp, wup_ref[:, :ML_E], preferred_element_type=F32)
    ext_ref[halo:halo + tm, :] = xm
    ext_ref[halo + tm:, :] = keep_next * jnp.dot(hn, wup_ref[:, :ML_E], preferred_element_type=F32)

    xc = jnp.zeros((tm, ML_E), F32) + cb_ref[...]
    for j in range(ML_CONV_K):
        xc = xc + cw_ref[j:j + 1, :] * ext_ref[halo - pad + j:halo - pad + j + tm, :]
    xc = _silu(xc)
    xcb = xc.astype(BF16)
    xmb = xm.astype(BF16)
    xc_ref[...] = xcb

    gates = jnp.zeros((tm, LANES), F32) + bg_ref[...]
    w = ML_QKV_BD
    for blk in range(ML_E // w):
        cols = slice(blk * w, (blk + 1) * w)
        qb = jnp.dot(xcb[:, cols], wq_ref[blk], preferred_element_type=F32).astype(BF16)
        kb = jnp.dot(xcb[:, cols], wk_ref[blk], preferred_element_type=F32).astype(BF16)
        vb = jnp.dot(xmb[:, cols], wv_ref[blk], preferred_element_type=F32).astype(BF16)
        q_ref[:, cols] = qb
        k_ref[:, cols] = kb
        v_ref[:, cols] = vb
        gates = gates + jnp.dot(qb, wg_ref[0, cols, :], preferred_element_type=F32)
        gates = gates + jnp.dot(kb, wg_ref[1, cols, :], preferred_element_type=F32)
        gates = gates + jnp.dot(vb, wg_ref[2, cols, :], preferred_element_type=F32)

    lf = jnp.minimum(gates, 0.0) - jnp.log1p(jnp.exp(-jnp.abs(gates)))
    r = lax.broadcasted_iota(jnp.int32, (ML_CHUNK, ML_CHUNK), 0)
    cidx = lax.broadcasted_iota(jnp.int32, (ML_CHUNK, ML_CHUNK), 1)
    tril = jnp.where(cidx <= r, 1.0, 0.0).astype(BF16)
    triu = jnp.where(cidx >= r, 1.0, 0.0).astype(BF16)
    lane = lax.broadcasted_iota(jnp.int32, (ML_CHUNK, LANES), 1)
    is_f_fwd = (lane >= ML_HEADS) & (lane < 2 * ML_HEADS)
    is_f_bwd = (lane >= 3 * ML_HEADS) & (lane < 4 * ML_HEADS)
    for n in range(tm // ML_CHUNK):
        rows = slice(n * ML_CHUNK, (n + 1) * ML_CHUNK)
        parts = _split3_bf16(lf[rows, :])
        cum_f = sum(jnp.dot(tril, part, preferred_element_type=F32) for part in parts)
        cum_b = sum(jnp.dot(triu, part, preferred_element_type=F32) for part in parts)
        gt_ref[rows, :] = jnp.where(is_f_fwd, cum_f, jnp.where(is_f_bwd, cum_b, gates[rows, :]))


def _block_diag(w, width):
    nblk, bj, bi = w.shape
    per = width // bj
    wt = w.reshape(nblk // per, per, bj, bi)
    eye = jnp.eye(per, dtype=w.dtype)
    dense = jnp.einsum("tpji,pq->tpjqi", wt, eye)
    return dense.reshape(nblk // per, width, width).astype(BF16)


def _ml_pre(x, mod_l, p, seg, tp, sp, ss):
    t = x.shape[0]
    tm = min(ML_PRE_TM, seg)
    nb8 = t // SUBLANES
    w_up = p["w_up"].astype(BF16)
    conv_w = p["conv_w"]
    conv_b = p["conv_b"].reshape(1, ML_E)
    wq = _block_diag(p["w_q"], ML_QKV_BD)
    wk = _block_diag(p["w_k"], ML_QKV_BD)
    wv = _block_diag(p["w_v"], ML_QKV_BD)
    ng = 4 * ML_HEADS
    wg = jnp.pad(p["w_gates"], ((0, 0), (0, LANES - ng))).reshape(3, ML_E, LANES).astype(BF16)
    bg = jnp.pad(p["b_gates"], (0, LANES - ng)).reshape(1, LANES)
    tok_bf16 = jax.ShapeDtypeStruct((t, ML_E), BF16)
    tok_spec = pl.BlockSpec((tm, ML_E), lambda i: (i, 0))
    return pl.pallas_call(
        functools.partial(_ml_pre_kernel, tp=tp, sp=sp, ss=ss),
        grid=(t // tm,),
        in_specs=[
            pl.BlockSpec((tm, D_MODEL), lambda i: (i, 0)),
            pl.BlockSpec((SUBLANES, D_MODEL),
                         lambda i: (jnp.maximum(i * (tm // SUBLANES) - 1, 0), 0)),
            pl.BlockSpec((SUBLANES, D_MODEL),
                         lambda i: (jnp.minimum((i + 1) * (tm // SUBLANES), nb8 - 1), 0)),
            pl.BlockSpec((None, 9, D_MODEL), lambda i: (i * tm // seg, 0, 0)),
            _resident(w_up.shape),
            _resident(conv_w.shape),
            _resident(conv_b.shape),
            _resident(wq.shape),
            _resident(wk.shape),
            _resident(wv.shape),
            _resident(wg.shape),
            _resident(bg.shape),
        ],
        out_specs=[tok_spec, tok_spec, tok_spec, tok_spec, tok_spec,
                   pl.BlockSpec((tm, LANES), lambda i: (i, 0))],
        out_shape=[tok_bf16, tok_bf16, tok_bf16, tok_bf16, tok_bf16,
                   jax.ShapeDtypeStruct((t, LANES), F32)],
        scratch_shapes=[pltpu.VMEM((tm + 2 * SUBLANES, ML_E), F32)],
        compiler_params=_params(1),
        name="mlstm_pre",
    )(x, x, x, mod_l, w_up, conv_w, conv_b, wq, wk, wv, wg, bg)


def _ml_scan_dir(q_ref, k_ref, v_ref, gc_ref, gr_ref, h_ref, c_ref, n_ref, m_ref, *, d, first):
    L = ML_CHUNK
    scale = ML_DH ** -0.5

    @pl.when(first)
    def _():
        c_ref[d] = jnp.zeros(c_ref.shape[1:], F32)
        n_ref[d] = jnp.zeros(n_ref.shape[1:], F32)
        m_ref[d] = jnp.full(m_ref.shape[1:], -jnp.inf, F32)

    r = lax.broadcasted_iota(jnp.int32, (L, L), 0)
    c = lax.broadcasted_iota(jnp.int32, (L, L), 1)
    mask = (c <= r) if d == 0 else (c >= r)
    for hd in range(ML_HEADS):
        li = 2 * ML_HEADS * d + hd
        la = li + ML_HEADS
        cols = slice(hd * ML_DH, (hd + 1) * ML_DH)
        i_c = gc_ref[:, li:li + 1]
        a_c = gc_ref[:, la:la + 1]
        i_r = gr_ref[li:li + 1, :]
        a_r = gr_ref[la:la + 1, :]
        total = a_c[L - 1:L, :] if d == 0 else a_c[0:1, :]
        m_prev = m_ref[d, hd][:, 0:1]
        qh = q_ref[:, cols]
        kh = k_ref[:, cols]
        vh = v_ref[:, cols]

        dmat = jnp.where(mask, a_c - a_r + i_r, -jnp.inf)
        inter = a_c + m_prev
        mj = jnp.maximum(inter, jnp.max(dmat, axis=-1, keepdims=True))
        w_inter = jnp.exp(inter - mj)
        s = lax.dot_general(qh, kh, (((1,), (1,)), ((), ())), preferred_element_type=F32)
        s = s * (jnp.exp(dmat - mj) * scale)
        num = (w_inter * jnp.dot(qh, c_ref[d, hd].astype(BF16), preferred_element_type=F32)
               + jnp.dot(s.astype(BF16), vh, preferred_element_type=F32))
        qn = jnp.sum(qh.astype(F32) * n_ref[d, hd], axis=-1, keepdims=True)
        den = w_inter * qn + jnp.sum(s, axis=-1, keepdims=True)
        h_ref[:, cols] = (num / jnp.maximum(jnp.abs(den), jnp.exp(-mj))).astype(BF16)

        dec = total - a_c + i_c
        m_new = jnp.maximum(total + m_prev, jnp.max(dec, axis=0, keepdims=True))
        keep = jnp.exp(total + m_prev - m_new)
        kw = kh.astype(F32) * (jnp.exp(dec - m_new) * scale)
        c_ref[d, hd] = keep * c_ref[d, hd] + lax.dot_general(
            kw.astype(BF16), vh, (((0,), (0,)), ((), ())), preferred_element_type=F32)
        n_ref[d, hd] = keep * n_ref[d, hd] + jnp.sum(kw, axis=0, keepdims=True)
        m_ref[d, hd] = jnp.broadcast_to(m_new, m_ref.shape[2:])


def _ml_scan_kernel(qf_ref, kf_ref, vf_ref, gcf_ref, grf_ref, qb_ref, kb_ref, vb_ref, gcb_ref,
                    grb_ref, hf_ref, hb_ref, c_ref, n_ref, m_ref, *, nch, tpc, ncp, ncs):
    cf = pl.program_id(0)
    cb = nch - 1 - cf

    def chunk_in_seq(ci):
        in_p = ci < tpc
        return (jnp.where(in_p, lax.rem(ci, ncp), lax.rem(ci - tpc, ncs)),
                jnp.where(in_p, ncp, ncs))

    pos_f, _ = chunk_in_seq(cf)
    pos_b, n_b = chunk_in_seq(cb)
    _ml_scan_dir(qf_ref, kf_ref, vf_ref, gcf_ref, grf_ref, hf_ref, c_ref, n_ref, m_ref,
                 d=0, first=pos_f == 0)
    _ml_scan_dir(qb_ref, kb_ref, vb_ref, gcb_ref, grb_ref, hb_ref, c_ref, n_ref, m_ref,
                 d=1, first=pos_b == n_b - 1)


def _ml_scan(q, k, v, gt, tp, sp, ss):
    t = q.shape[0]
    L = ML_CHUNK
    nch = t // L
    ng = 4 * ML_HEADS
    gt_rows = gt[:, :ng].T
    fwd = lambda i: (i, 0)
    bwd = lambda i: (nch - 1 - i, 0)
    fwd_t = lambda i: (0, i)
    bwd_t = lambda i: (0, nch - 1 - i)
    tok = lambda m: pl.BlockSpec((L, ML_E), m)
    out = jax.ShapeDtypeStruct((t, ML_E), BF16)
    return pl.pallas_call(
        functools.partial(_ml_scan_kernel, nch=nch, tpc=tp // L, ncp=sp // L, ncs=ss // L),
        grid=(nch,),
        in_specs=[tok(fwd), tok(fwd), tok(fwd), pl.BlockSpec((L, LANES), fwd),
                  pl.BlockSpec((ng, L), fwd_t),
                  tok(bwd), tok(bwd), tok(bwd), pl.BlockSpec((L, LANES), bwd),
                  pl.BlockSpec((ng, L), bwd_t)],
        out_specs=[tok(fwd), tok(bwd)],
        out_shape=[out, out],
        scratch_shapes=[pltpu.VMEM((2, ML_HEADS, ML_DH, ML_DH), F32),
                        pltpu.VMEM((2, ML_HEADS, 1, ML_DH), F32),
                        pltpu.VMEM((2, ML_HEADS, 1, LANES), F32)],
        compiler_params=_params(1),
        name="mlstm_scan",
    )(q, k, v, gt, gt_rows, q, k, v, gt, gt_rows)


def _ml_post_kernel(x_ref, mod_ref, hf_ref, hb_ref, xc_ref, z_ref, gn_ref, skip_ref, wd_ref,
                    g_ref, b_ref, o_ref, y_ref):
    for hd in range(ML_HEADS):
        cols = slice(hd * ML_DH, (hd + 1) * ML_DH)
        hs = hf_ref[:, cols].astype(F32) + hb_ref[:, cols].astype(F32)
        mu = jnp.mean(hs, axis=-1, keepdims=True)
        c = hs - mu
        var = jnp.mean(c * c, axis=-1, keepdims=True)
        hn = c * lax.rsqrt(var + NORM_EPS) * gn_ref[:, cols]
        y = (hn + skip_ref[:, cols] * xc_ref[:, cols].astype(F32)) * _silu(z_ref[:, cols].astype(F32))
        y_ref[:, cols] = y.astype(BF16)
    out = jnp.dot(y_ref[...], wd_ref[...], preferred_element_type=F32)
    o_ref[...] = _close(x_ref[...], out, mod_ref, 1, g_ref, b_ref)


def _ml_post(x, mod_l, hf, hb, xc, z, p, g, b, seg):
    t = x.shape[0]
    tm = min(ML_POST_TM, seg)
    gn = p["gn_g"].reshape(1, ML_E)
    skip = p["skip"].reshape(1, ML_E)
    w_down = p["w_down"].astype(BF16)
    tok = pl.BlockSpec((tm, ML_E), lambda i: (i, 0))
    return pl.pallas_call(
        _ml_post_kernel,
        grid=(t // tm,),
        in_specs=[
            pl.BlockSpec((tm, D_MODEL), lambda i: (i, 0)),
            pl.BlockSpec((None, 9, D_MODEL), lambda i: (i * tm // seg, 0, 0)),
            tok, tok, tok, tok,
            _resident(gn.shape), _resident(skip.shape), _resident(w_down.shape),
            _resident((1, D_MODEL)), _resident((1, D_MODEL)),
        ],
        out_specs=pl.BlockSpec((tm, D_MODEL), lambda i: (i, 0)),
        out_shape=jax.ShapeDtypeStruct((t, D_MODEL), F32),
        scratch_shapes=[pltpu.VMEM((tm, ML_E), BF16)],
        compiler_params=_params(1),
        name="mlstm_post",
    )(x, mod_l, hf, hb, xc, z, gn, skip, w_down, g, b)


def _rms(x, g_ref):
    return (x * lax.rsqrt(jnp.mean(x * x, axis=-1, keepdims=True) + NORM_EPS) * g_ref[...]).astype(BF16)


def _mla_pre_kernel(x_ref, mod_ref, cos_ref, sin_ref, wd_ref, qn_ref, kvn_ref, wq_ref, wkv_ref,
                    q_ref, k_ref, v_ref):
    scale = (MLA_NOPE + MLA_ROPE) ** -0.5
    hw = MLA_HEADS * LANES
    h = _modulated(x_ref[...], mod_ref, 1)
    c = jnp.dot(h, wd_ref[...], preferred_element_type=F32)
    cq = _rms(c[:, :MLA_Q_LORA], qn_ref)
    kv0 = MLA_Q_LORA + MLA_KV_LORA
    ckv = _rms(c[:, MLA_Q_LORA:kv0], kvn_ref)
    cos = cos_ref[...]
    sin = sin_ref[...]
    k_rope = (c[:, kv0:kv0 + LANES] * cos + c[:, kv0 + LANES:kv0 + 2 * LANES] * sin).astype(BF16)
    q = jnp.dot(cq, wq_ref[...], preferred_element_type=F32)
    kv = jnp.dot(ckv, wkv_ref[...], preferred_element_type=F32)
    v_ref[...] = kv[:, hw:].astype(BF16)
    for hd in range(MLA_HEADS):
        blk = slice(hd * LANES, (hd + 1) * LANES)
        rot = q[:, hw + hd * LANES:hw + (hd + 1) * LANES] * cos
        rot = rot + q[:, 2 * hw + hd * LANES:2 * hw + (hd + 1) * LANES] * sin
        o0 = hd * MLA_QK_W
        q_ref[:, o0:o0 + LANES] = (q[:, blk] * scale).astype(BF16)
        q_ref[:, o0 + LANES:o0 + 2 * LANES] = (rot * scale).astype(BF16)
        k_ref[:, o0:o0 + LANES] = kv[:, blk].astype(BF16)
        k_ref[:, o0 + LANES:o0 + 2 * LANES] = k_rope


def _rope_tables(smax):
    half = MLA_ROPE // 2
    inv = ROPE_THETA ** (-jnp.arange(0, MLA_ROPE, 2, dtype=F32) / MLA_ROPE)
    ang = jnp.arange(smax, dtype=F32)[:, None] * inv[None, :]
    z = jnp.zeros((smax, LANES - 2 * half), F32)
    cos = jnp.concatenate([jnp.cos(ang), jnp.cos(ang), z], axis=-1)
    sin = jnp.concatenate([jnp.sin(ang), jnp.sin(ang), z], axis=-1)
    return cos, sin


def _mla_pre(x, mod_l, p, seg, tp, sp, ss):
    t = x.shape[0]
    tm = min(MLA_PRE_TM, seg)
    half = MLA_ROPE // 2
    kv0 = MLA_Q_LORA + MLA_KV_LORA
    wd = p["w_down"]
    x1, x2 = wd[:, kv0:kv0 + half], wd[:, kv0 + half:]
    zpad = jnp.zeros((D_MODEL, LANES - 2 * half), F32)
    wd_ext = jnp.concatenate([wd[:, :kv0], x1, x2, zpad, -x2, x1, zpad], axis=-1).astype(BF16)
    wuq = p["w_uq"].reshape(MLA_Q_LORA, MLA_HEADS, MLA_NOPE + MLA_ROPE)
    q1, q2 = wuq[..., MLA_NOPE:MLA_NOPE + half], wuq[..., MLA_NOPE + half:]
    qz = jnp.zeros((MLA_Q_LORA, MLA_HEADS, LANES - 2 * half), F32)
    wq_all = jnp.concatenate([
        wuq[..., :MLA_NOPE].reshape(MLA_Q_LORA, -1),
        jnp.concatenate([q1, q2, qz], axis=-1).reshape(MLA_Q_LORA, -1),
        jnp.concatenate([-q2, q1, qz], axis=-1).reshape(MLA_Q_LORA, -1)], axis=-1).astype(BF16)
    wukv = p["w_ukv"].reshape(MLA_KV_LORA, MLA_HEADS, MLA_NOPE + MLA_V)
    wkv_all = jnp.concatenate([wukv[..., :MLA_NOPE].reshape(MLA_KV_LORA, -1),
                               wukv[..., MLA_NOPE:].reshape(MLA_KV_LORA, -1)], axis=-1).astype(BF16)
    qn = p["q_norm"].reshape(1, MLA_Q_LORA)
    kvn = p["kv_norm"].reshape(1, MLA_KV_LORA)
    cos, sin = _rope_tables(max(sp, ss))

    def pos_block(i):
        pos, _ = _seq_pos(i * tm, tp, sp, ss)
        return (pos // tm, 0)

    return pl.pallas_call(
        _mla_pre_kernel,
        grid=(t // tm,),
        in_specs=[
            pl.BlockSpec((tm, D_MODEL), lambda i: (i, 0)),
            pl.BlockSpec((None, 9, D_MODEL), lambda i: (i * tm // seg, 0, 0)),
            pl.BlockSpec((tm, LANES), pos_block),
            pl.BlockSpec((tm, LANES), pos_block),
            _resident(wd_ext.shape), _resident(qn.shape), _resident(kvn.shape),
            _resident(wq_all.shape), _resident(wkv_all.shape),
        ],
        out_specs=[pl.BlockSpec((tm, MLA_HEADS * MLA_QK_W), lambda i: (i, 0)),
                   pl.BlockSpec((tm, MLA_HEADS * MLA_QK_W), lambda i: (i, 0)),
                   pl.BlockSpec((tm, MLA_HEADS * MLA_V), lambda i: (i, 0))],
        out_shape=[jax.ShapeDtypeStruct((t, MLA_HEADS * MLA_QK_W), BF16),
                   jax.ShapeDtypeStruct((t, MLA_HEADS * MLA_QK_W), BF16),
                   jax.ShapeDtypeStruct((t, MLA_HEADS * MLA_V), BF16)],
        compiler_params=_params(1),
        name="mla_pre",
    )(x, mod_l, cos, sin, wd_ext, qn, kvn, wq_all, wkv_all)


def _attn_kernel(q_ref, k_ref, v_ref, o_ref, *, kc):
    q = q_ref[...]
    s_len = k_ref.shape[0]
    m = l = acc = None
    for j in range(s_len // kc):
        rows = slice(j * kc, (j + 1) * kc)
        s = lax.dot_general(q, k_ref[rows, :], (((1,), (1,)), ((), ())), preferred_element_type=F32)
        m_j = jnp.max(s, axis=-1, keepdims=True)
        if j == 0:
            m = m_j
            p = jnp.exp(s - m)
            l = jnp.sum(p, axis=-1, keepdims=True)
            acc = jnp.dot(p.astype(BF16), v_ref[rows, :], preferred_element_type=F32)
        else:
            m_new = jnp.maximum(m, m_j)
            alpha = jnp.exp(m - m_new)
            p = jnp.exp(s - m_new)
            l = alpha * l + jnp.sum(p, axis=-1, keepdims=True)
            acc = alpha * acc + jnp.dot(p.astype(BF16), v_ref[rows, :], preferred_element_type=F32)
            m = m_new
    o_ref[...] = (acc / l).astype(BF16)


def _attention(q, k, v, row0, nb, s_len):
    tq = min(ATTN_TQ, s_len)
    kc = min(ATTN_KC, s_len)
    nq = s_len // tq
    qb0 = row0 // tq
    sb0 = row0 // s_len
    return pl.pallas_call(
        functools.partial(_attn_kernel, kc=kc),
        grid=(nb, MLA_HEADS, nq),
        in_specs=[
            pl.BlockSpec((tq, MLA_QK_W), lambda b, h, i: (qb0 + b * nq + i, h)),
            pl.BlockSpec((s_len, MLA_QK_W), lambda b, h, i: (sb0 + b, h)),
            pl.BlockSpec((s_len, MLA_V), lambda b, h, i: (sb0 + b, h)),
        ],
        out_specs=pl.BlockSpec((tq, MLA_V), lambda b, h, i: (b * nq + i, h)),
        out_shape=jax.ShapeDtypeStruct((nb * s_len, MLA_HEADS * MLA_V), BF16),
        compiler_params=_params(3),
        name="mla_attention",
    )(q, k, v)


def _mla_post_kernel(x_ref, mod_ref, o_in_ref, wo_ref, g_ref, b_ref, o_ref):
    y = jnp.dot(o_in_ref[...], wo_ref[...], preferred_element_type=F32)
    o_ref[...] = _close(x_ref[...], y, mod_ref, 1, g_ref, b_ref)


def _mla_post(x, mod_l, o, w_o, g, b, seg):
    t = x.shape[0]
    tm = min(MLA_POST_TM, seg)
    w_o = w_o.astype(BF16)
    return pl.pallas_call(
        _mla_post_kernel,
        grid=(t // tm,),
        in_specs=[
            pl.BlockSpec((tm, D_MODEL), lambda i: (i, 0)),
            pl.BlockSpec((None, 9, D_MODEL), lambda i: (i * tm // seg, 0, 0)),
            pl.BlockSpec((tm, MLA_HEADS * MLA_V), lambda i: (i, 0)),
            _resident(w_o.shape), _resident((1, D_MODEL)), _resident((1, D_MODEL)),
        ],
        out_specs=pl.BlockSpec((tm, D_MODEL), lambda i: (i, 0)),
        out_shape=jax.ShapeDtypeStruct((t, D_MODEL), F32),
        compiler_params=_params(1),
        name="mla_post",
    )(x, mod_l, o, w_o, g, b)


def kernel(x_prompt, x_sample, c_prompt, c_sample, ada_w, ada_b, ln_g, ln_b, ffn_w_in, ffn_w_out, gm_w_in, gm_b_in, gm_ln_g, gm_ln_b, gm_w_s, gm_b_s, gm_w_out, ml_w_up, ml_conv_w, ml_conv_b, ml_w_q, ml_w_k, ml_w_v, ml_w_gates, ml_b_gates, ml_gn_g, ml_skip, ml_w_down, mla_w_down, mla_q_norm, mla_kv_norm, mla_w_uq, mla_w_ukv, mla_w_o):
    bp, sp, d = x_prompt.shape
    bs, ss, _ = x_sample.shape
    tp = bp * sp
    seg = math.gcd(sp, ss)
    assert d == D_MODEL and seg % ML_CHUNK == 0 and tp % ss == 0 and tp % sp == 0

    x = jnp.concatenate([x_prompt.reshape(tp, d), x_sample.reshape(bs * ss, d)], axis=0)
    cond = jnp.concatenate([c_prompt, c_sample], axis=0)
    mod = _ada_table(cond, ada_w, ada_b)
    seg_seq = np.concatenate([np.repeat(np.arange(bp), sp // seg),
                              bp + np.repeat(np.arange(bs), ss // seg)])
    mod = mod[:, seg_seq].reshape(DEPTH, len(seg_seq), 9, d)

    counts = [0, 0, 0]
    for layer in range(DEPTH):
        mod_l = mod[layer]
        lg = lambda j: ln_g[layer, j].reshape(1, d)
        lb = lambda j: ln_b[layer, j].reshape(1, d)
        w_in_r, w_out_r = _prep_ffn(ffn_w_in[layer, 0], ffn_w_out[layer, 0])
        x = _ffn_sublayer(x, mod_l, 0, w_in_r, w_out_r, lg(0), lb(0), seg)
        kind = layer % N_MIXERS
        i = counts[kind]
        counts[kind] += 1
        if kind == 0:
            p = dict(w_in=gm_w_in[i], b_in=gm_b_in[i], ln_g=gm_ln_g[i], ln_b=gm_ln_b[i],
                     w_s=gm_w_s[i], b_s=gm_b_s[i], w_out=gm_w_out[i])
            x = _gmlp_sublayer(x, mod_l, p, lg(1), lb(1), seg)
        elif kind == 1:
            p = dict(w_up=ml_w_up[i], conv_w=ml_conv_w[i], conv_b=ml_conv_b[i], w_q=ml_w_q[i],
                     w_k=ml_w_k[i], w_v=ml_w_v[i], w_gates=ml_w_gates[i], b_gates=ml_b_gates[i],
                     gn_g=ml_gn_g[i], skip=ml_skip[i], w_down=ml_w_down[i])
            q, k, v, xc, z, gt = _ml_pre(x, mod_l, p, seg, tp, sp, ss)
            hf, hb = _ml_scan(q, k, v, gt, tp, sp, ss)
            x = _ml_post(x, mod_l, hf, hb, xc, z, p, lg(1), lb(1), seg)
        else:
            p = dict(w_down=mla_w_down[i], q_norm=mla_q_norm[i], kv_norm=mla_kv_norm[i],
                     w_uq=mla_w_uq[i], w_ukv=mla_w_ukv[i])
            q, k, v = _mla_pre(x, mod_l, p, seg, tp, sp, ss)
            o = jnp.concatenate([_attention(q, k, v, 0, bp, sp),
                                 _attention(q, k, v, tp, bs, ss)], axis=0)
            x = _mla_post(x, mod_l, o, mla_w_o[i], lg(1), lb(1), seg)
        w_in_r, w_out_r = _prep_ffn(ffn_w_in[layer, 1], ffn_w_out[layer, 1])
        x = _ffn_sublayer(x, mod_l, 2, w_in_r, w_out_r, lg(2), lb(2), seg)
    return x[:tp].reshape(bp, sp, d), x[tp:].reshape(bs, ss, d)
```

```python
import functools
import math

import jax
import jax.numpy as jnp
import numpy as np
from jax import lax
from jax.experimental import pallas as pl
from jax.experimental.pallas import tpu as pltpu

F32 = jnp.float32
BF16 = jnp.bfloat16

D_MODEL = 1024
DEPTH = 4
N_MIXERS = 3
D_FF = 2816
GM_E = 3 * D_MODEL
GM_CHUNK = 128
GM_GROUP_W = 128
GM_GROUPS = GM_E // GM_GROUP_W
ML_E = 2 * D_MODEL
ML_HEADS = 4
ML_DH = ML_E // ML_HEADS
ML_CONV_K = 5
ML_QKV_BLOCK = 4
MLA_HEADS = 8
MLA_NOPE = 128
MLA_ROPE = 64
MLA_V = 128
MLA_Q_LORA = 384
MLA_KV_LORA = 256
ROPE_THETA = 10000.0
LN_EPS = 1e-5
NORM_EPS = 1e-6
DEEPNORM_ALPHA = (2 * DEPTH) ** 0.25
LOG2_E = 1.4426950408889634

LANES = 128
SUBLANES = 8
MXU_DIM = 256
VMEM_LIMIT_BYTES = 56 * 1024 * 1024

FFN_TM = 512
FFN_SPLITS = (0, 6 * MXU_DIM, D_FF)
GMLP_TM = 512
GMLP_SUB = 256
ML_PRE_TM = 256
ML_SCAN_L = 256
ML_POST_TM = 512
MLA_PRE_TM = 512
MLA_POST_TM = 512
ATTN_TQ = 1024
ATTN_ROW_CHAINS = 2
ATTN_KC = 2048
ADA_TN = 1152
ML_QKV_BD = MXU_DIM
MLA_QK_W = 2 * LANES


def _params(n_grid_axes):
    return pltpu.CompilerParams(
        dimension_semantics=("arbitrary",) * n_grid_axes,
        vmem_limit_bytes=VMEM_LIMIT_BYTES,
    )


def _resident(shape):
    nd = len(shape)
    return pl.BlockSpec(shape, lambda *_: (0,) * nd, pipeline_mode=pl.Buffered(1))


def _two_source_specs(first, second, tm, width):
    n_first = first.shape[0] // tm
    return n_first, [
        pl.BlockSpec((tm, width), lambda i: (jnp.minimum(i, n_first - 1), 0)),
        pl.BlockSpec((tm, width), lambda i: (jnp.maximum(i - n_first, 0), 0))]


def _modulated(x, mod_ref, j):
    shift = mod_ref[3 * j:3 * j + 1, :]
    scale = mod_ref[3 * j + 1:3 * j + 2, :]
    return (x * (1.0 + scale) + shift).astype(BF16)


def _close(x, y, mod_ref, j, g_ref, b_ref):
    gate = mod_ref[3 * j + 2:3 * j + 3, :]
    v = DEEPNORM_ALPHA * x + gate * y
    mu = jnp.mean(v, axis=-1, keepdims=True)
    c = v - mu
    var = jnp.mean(c * c, axis=-1, keepdims=True)
    return c * lax.rsqrt(var + LN_EPS) * g_ref[...] + b_ref[...]


def _silu(x):
    return x * jax.nn.sigmoid(x)


def _ada_kernel(c_ref, w_ref, b_ref, o_ref):
    a = _silu(c_ref[...]).astype(BF16)
    o_ref[...] = jnp.dot(a, w_ref[...].astype(BF16), preferred_element_type=F32) + b_ref[...]


def _ada_table(cond, ada_w, ada_b):
    ns = cond.shape[0]
    nd = ada_w.shape[-1]
    return pl.pallas_call(
        _ada_kernel,
        grid=(DEPTH, nd // ADA_TN),
        in_specs=[
            pl.BlockSpec((ns, D_MODEL), lambda l, n: (0, 0)),
            pl.BlockSpec((None, D_MODEL, ADA_TN), lambda l, n: (l, 0, n)),
            pl.BlockSpec((None, 1, ADA_TN), lambda l, n: (l, 0, n)),
        ],
        out_specs=pl.BlockSpec((None, ns, ADA_TN), lambda l, n: (l, 0, n)),
        out_shape=jax.ShapeDtypeStruct((DEPTH, ns, nd), F32),
        compiler_params=_params(2),
        name="ada_table",
    )(cond, ada_w, ada_b.reshape(DEPTH, 1, nd))


def _ffn_kernel(*refs, j, n_first):
    *x_refs, mod_ref, win_ref, wout_ref, g_ref, b_ref, o_ref = refs
    if len(x_refs) == 2:
        x = jnp.where(pl.program_id(0) < n_first, x_refs[0][...], x_refs[1][...])
    else:
        x = x_refs[0][...]
    h = _modulated(x, mod_ref, j)
    acc = jnp.zeros(x.shape, F32)
    for c0, c1 in zip(FFN_SPLITS[:-1], FFN_SPLITS[1:]):
        gate = jnp.dot(h, win_ref[:, c0:c1], preferred_element_type=F32)
        up = jnp.dot(h, win_ref[:, D_FF + c0:D_FF + c1], preferred_element_type=F32)
        a = (_silu(gate) * up).astype(BF16)
        acc = acc + jnp.dot(a, wout_ref[c0:c1, :], preferred_element_type=F32)
    o_ref[...] = _close(x, 0.5 * acc, mod_ref, j, g_ref, b_ref)


def _ffn_sublayer(xs, mod_l, j, w_in, w_out, g, b, seg, row0=0, rows=None):
    tm = min(FFN_TM, seg)
    if isinstance(xs, tuple):
        rows = xs[0].shape[0] + xs[1].shape[0]
        n_first, x_specs = _two_source_specs(xs[0], xs[1], tm, D_MODEL)
    else:
        rows = xs.shape[0] - row0 if rows is None else rows
        n_first, x_specs = 0, [pl.BlockSpec((tm, D_MODEL), lambda i: (row0 // tm + i, 0))]
        xs = (xs,)
    return pl.pallas_call(
        functools.partial(_ffn_kernel, j=j, n_first=n_first),
        grid=(rows // tm,),
        in_specs=x_specs + [
            pl.BlockSpec((None, 9, D_MODEL), lambda i: ((row0 + i * tm) // seg, 0, 0)),
            _resident(w_in.shape),
            _resident(w_out.shape),
            _resident((1, D_MODEL)),
            _resident((1, D_MODEL)),
        ],
        out_specs=pl.BlockSpec((tm, D_MODEL), lambda i: (i, 0)),
        out_shape=jax.ShapeDtypeStruct((rows, D_MODEL), F32),
        compiler_params=_params(1),
        name="ffn_sublayer",
    )(*xs, mod_l, w_in, w_out, g, b)


def _gelu(x):
    return 0.5 * x * (1.0 + lax.erf(x * (2.0 ** -0.5)))


def _gmlp_kernel(x_ref, mod_ref, win_ref, bin_ref, lng_ref, lnb_ref, ws_ref, bs_ref,
                 wout_ref, g_ref, b_ref, o_ref, vb_ref, mix_ref):
    for t0 in range(0, x_ref.shape[0], GMLP_SUB):
        sub = slice(t0, t0 + GMLP_SUB)
        x = x_ref[sub, :]
        h = _modulated(x, mod_ref, 1)
        v = _gelu(jnp.dot(h, win_ref[:, GM_E:], preferred_element_type=F32) + bin_ref[:, GM_E:])
        mu = jnp.mean(v, axis=-1, keepdims=True)
        c = v - mu
        var = jnp.mean(c * c, axis=-1, keepdims=True)
        vb_ref[sub, :] = (c * lax.rsqrt(var + LN_EPS) * lng_ref[...] + lnb_ref[...]).astype(BF16)
        for n in range(t0, t0 + GMLP_SUB, GM_CHUNK):
            rows = slice(n, n + GM_CHUNK)
            for grp in range(GM_GROUPS):
                cols = slice(grp * GM_GROUP_W, (grp + 1) * GM_GROUP_W)
                mix_ref[rows, cols] = (
                    jnp.dot(ws_ref[grp], vb_ref[rows, cols], preferred_element_type=F32)
                    + bs_ref[:, cols])
        u = _gelu(jnp.dot(h, win_ref[:, :GM_E], preferred_element_type=F32) + bin_ref[:, :GM_E])
        y = jnp.dot((u * mix_ref[sub, :]).astype(BF16), wout_ref[...], preferred_element_type=F32)
        o_ref[sub, :] = _close(x, y, mod_ref, 1, g_ref, b_ref)


def _gmlp_sublayer(x, mod_l, p, g, b, seg):
    t = x.shape[0]
    tm = min(GMLP_TM, seg)
    w_in = p["w_in"].astype(BF16)
    b_in = p["b_in"].reshape(1, 2 * GM_E)
    ln_g = p["ln_g"].reshape(1, GM_E)
    ln_b = p["ln_b"].reshape(1, GM_E)
    w_s = p["w_s"].astype(BF16)
    bs_full = jnp.repeat(p["b_s"].T, GM_GROUP_W, axis=1)
    w_out = p["w_out"].astype(BF16)
    return pl.pallas_call(
        _gmlp_kernel,
        grid=(t // tm,),
        in_specs=[
            pl.BlockSpec((tm, D_MODEL), lambda i: (i, 0)),
            pl.BlockSpec((None, 9, D_MODEL), lambda i: (i * tm // seg, 0, 0)),
            _resident(w_in.shape),
            _resident(b_in.shape),
            _resident(ln_g.shape),
            _resident(ln_b.shape),
            _resident(w_s.shape),
            _resident(bs_full.shape),
            _resident(w_out.shape),
            _resident((1, D_MODEL)),
            _resident((1, D_MODEL)),
        ],
        out_specs=pl.BlockSpec((tm, D_MODEL), lambda i: (i, 0)),
        out_shape=jax.ShapeDtypeStruct((t, D_MODEL), F32),
        scratch_shapes=[pltpu.VMEM((tm, GM_E), BF16), pltpu.VMEM((tm, GM_E), F32)],
        compiler_params=_params(1),
        name="gmlp_sublayer",
    )(x, mod_l, w_in, b_in, ln_g, ln_b, w_s, bs_full, w_out, g, b)


def _seq_pos(row0, tp, sp, ss):
    in_p = row0 < tp
    pos = jnp.where(in_p, lax.rem(row0, sp), lax.rem(row0 - tp, ss))
    return pos, jnp.where(in_p, sp, ss)


def _split3_bf16(x):
    hi = x.astype(BF16)
    r1 = x - hi.astype(F32)
    mid = r1.astype(BF16)
    lo = (r1 - mid.astype(F32)).astype(BF16)
    return hi, mid, lo


def _ml_pre_kernel(x_ref, xp_ref, xn_ref, mod_ref, wup_ref, cw_ref, cb_ref, wq_ref, wk_ref,
                   wkt_ref, wv_ref, wg_ref, bg_ref, q_ref, k_ref, kt_ref, v_ref, xc_ref, z_ref,
                   gt_ref, ext_ref, *, tp, sp, ss):
    tm = x_ref.shape[0]
    halo = SUBLANES
    pad = ML_CONV_K // 2
    pos, slen = _seq_pos(pl.program_id(0) * tm, tp, sp, ss)
    keep_prev = jnp.where(pos == 0, 0.0, 1.0)
    keep_next = jnp.where(pos + tm == slen, 0.0, 1.0)

    h = _modulated(x_ref[...], mod_ref, 1)
    up = jnp.dot(h, wup_ref[...], preferred_element_type=F32)
    xm = up[:, :ML_E]
    z_ref[...] = up[:, ML_E:].astype(BF16)
    hp = _modulated(xp_ref[...], mod_ref, 1)
    hn = _modulated(xn_ref[...], mod_ref, 1)
    ext_ref[0:halo, :] = keep_prev * jnp.dot(hp, wup_ref[:, :ML_E], preferred_element_type=F32)
    ext_ref[halo:halo + tm, :] = xm
    ext_ref[halo + tm:, :] = keep_next * jnp.dot(hn, wup_ref[:, :ML_E], preferred_element_type=F32)

    xc = jnp.zeros((tm, ML_E), F32) + cb_ref[...]
    for j in range(ML_CONV_K):
        xc = xc + cw_ref[j:j + 1, :] * ext_ref[halo - pad + j:halo - pad + j + tm, :]
    xc = _silu(xc)
    xcb = xc.astype(BF16)
    xmb = xm.astype(BF16)
    xc_ref[...] = xcb

    gates = jnp.zeros((tm, LANES), F32) + bg_ref[...]
    w = ML_QKV_BD
    nt = (((1,), (1,)), ((), ()))
    for blk in range(ML_E // w):
        cols = slice(blk * w, (blk + 1) * w)
        qb = jnp.dot(xcb[:, cols], wq_ref[blk], preferred_element_type=F32).astype(BF16)
        kb = jnp.dot(xcb[:, cols], wk_ref[blk], preferred_element_type=F32).astype(BF16)
        vb = jnp.dot(xmb[:, cols], wv_ref[blk], preferred_element_type=F32).astype(BF16)
        q_ref[:, cols] = qb
        k_ref[:, cols] = kb
        v_ref[:, cols] = vb
        kt_ref[cols, :] = lax.dot_general(wkt_ref[blk], xcb[:, cols], nt,
                                          preferred_element_type=F32).astype(BF16)
        gates = gates + jnp.dot(qb, wg_ref[0, cols, :], preferred_element_type=F32)
        gates = gates + jnp.dot(kb, wg_ref[1, cols, :], preferred_element_type=F32)
        gates = gates + jnp.dot(vb, wg_ref[2, cols, :], preferred_element_type=F32)

    L = ML_SCAN_L
    lf = jnp.minimum(gates, 0.0) - jnp.log1p(jnp.exp(-jnp.abs(gates)))
    r = lax.broadcasted_iota(jnp.int32, (L, L), 0)
    cidx = lax.broadcasted_iota(jnp.int32, (L, L), 1)
    tril = jnp.where(cidx <= r, 1.0, 0.0).astype(BF16)
    triu = jnp.where(cidx >= r, 1.0, 0.0).astype(BF16)
    lane = lax.broadcasted_iota(jnp.int32, (L, LANES), 1)
    is_f_fwd = (lane >= ML_HEADS) & (lane < 2 * ML_HEADS)
    is_f_bwd = (lane >= 3 * ML_HEADS) & (lane < 4 * ML_HEADS)
    for n in range(tm // L):
        rows = slice(n * L, (n + 1) * L)
        parts = _split3_bf16(lf[rows, :])
        cum_f = sum(jnp.dot(tril, part, preferred_element_type=F32) for part in parts)
        cum_b = sum(jnp.dot(triu, part, preferred_element_type=F32) for part in parts)
        gt_ref[rows, :] = jnp.where(is_f_fwd, cum_f, jnp.where(is_f_bwd, cum_b, gates[rows, :]))


def _block_diag(w, width):
    nblk, bj, bi = w.shape
    per = width // bj
    wt = w.reshape(nblk // per, per, bj, bi)
    eye = jnp.eye(per, dtype=w.dtype)
    dense = jnp.einsum("tpji,pq->tpjqi", wt, eye)
    return dense.reshape(nblk // per, width, width).astype(BF16)


def _ml_pre(x, mod_l, p, seg, tp, sp, ss):
    t = x.shape[0]
    tm = min(ML_PRE_TM, seg)
    assert tm % ML_SCAN_L == 0
    nb8 = t // SUBLANES
    w_up = p["w_up"].astype(BF16)
    conv_w = p["conv_w"]
    conv_b = p["conv_b"].reshape(1, ML_E)
    wq = _block_diag(p["w_q"], ML_QKV_BD)
    wk = _block_diag(p["w_k"], ML_QKV_BD)
    wkt = wk.transpose(0, 2, 1)
    wv = _block_diag(p["w_v"], ML_QKV_BD)
    ng = 4 * ML_HEADS
    wg = jnp.pad(p["w_gates"], ((0, 0), (0, LANES - ng))).reshape(3, ML_E, LANES).astype(BF16)
    bg = jnp.pad(p["b_gates"], (0, LANES - ng)).reshape(1, LANES)
    tok_bf16 = jax.ShapeDtypeStruct((t, ML_E), BF16)
    tok_spec = pl.BlockSpec((tm, ML_E), lambda i: (i, 0))
    return pl.pallas_call(
        functools.partial(_ml_pre_kernel, tp=tp, sp=sp, ss=ss),
        grid=(t // tm,),
        in_specs=[
            pl.BlockSpec((tm, D_MODEL), lambda i: (i, 0)),
            pl.BlockSpec((SUBLANES, D_MODEL),
                         lambda i: (jnp.maximum(i * (tm // SUBLANES) - 1, 0), 0)),
            pl.BlockSpec((SUBLANES, D_MODEL),
                         lambda i: (jnp.minimum((i + 1) * (tm // SUBLANES), nb8 - 1), 0)),
            pl.BlockSpec((None, 9, D_MODEL), lambda i: (i * tm // seg, 0, 0)),
            _resident(w_up.shape),
            _resident(conv_w.shape),
            _resident(conv_b.shape),
            _resident(wq.shape),
            _resident(wk.shape),
            _resident(wkt.shape),
            _resident(wv.shape),
            _resident(wg.shape),
            _resident(bg.shape),
        ],
        out_specs=[tok_spec, tok_spec, pl.BlockSpec((ML_E, tm), lambda i: (0, i)),
                   tok_spec, tok_spec, tok_spec,
                   pl.BlockSpec((tm, LANES), lambda i: (i, 0))],
        out_shape=[tok_bf16, tok_bf16, jax.ShapeDtypeStruct((ML_E, t), BF16),
                   tok_bf16, tok_bf16, tok_bf16,
                   jax.ShapeDtypeStruct((t, LANES), F32)],
        scratch_shapes=[pltpu.VMEM((tm + 2 * SUBLANES, ML_E), F32)],
        compiler_params=_params(1),
        name="mlstm_pre",
    )(x, x, x, mod_l, w_up, conv_w, conv_b, wq, wk, wkt, wv, wg, bg)


def _ml_scan_dir(q_ref, k_ref, kt_ref, v_ref, gc_ref, gr_ref, h_ref, c_ref, cb_ref, n_ref, m_ref,
                 *, d, first):
    L = ML_SCAN_L
    scale = ML_DH ** -0.5

    @pl.when(first)
    def _():
        c_ref[d] = jnp.zeros(c_ref.shape[1:], F32)
        cb_ref[d] = jnp.zeros(cb_ref.shape[1:], BF16)
        n_ref[d] = jnp.zeros(n_ref.shape[1:], F32)
        m_ref[d] = jnp.full(m_ref.shape[1:], -jnp.inf, F32)

    r = lax.broadcasted_iota(jnp.int32, (L, L), 0)
    c = lax.broadcasted_iota(jnp.int32, (L, L), 1)
    mask = (c <= r) if d == 0 else (c >= r)
    for hd in range(ML_HEADS):
        li = 2 * ML_HEADS * d + hd
        la = li + ML_HEADS
        cols = slice(hd * ML_DH, (hd + 1) * ML_DH)
        i_c = gc_ref[:, li:li + 1]
        a_c = gc_ref[:, la:la + 1]
        i_r = gr_ref[li:li + 1, :]
        a_r = gr_ref[la:la + 1, :]
        total = a_c[L - 1:L, :] if d == 0 else a_c[0:1, :]
        m_prev = m_ref[d, hd][:, 0:1]
        qh = q_ref[:, cols]
        vh = v_ref[:, cols]
        kth = kt_ref[cols, :]

        dmat = jnp.where(mask, a_c - a_r + i_r, -jnp.inf)
        inter = a_c + m_prev
        mj = jnp.maximum(inter, jnp.max(dmat, axis=-1, keepdims=True))
        w_inter = jnp.exp(inter - mj)
        s = jnp.dot(qh, kth, preferred_element_type=F32) * (jnp.exp(dmat - mj) * scale)
        num = (w_inter * jnp.dot(qh, cb_ref[d, hd], preferred_element_type=F32)
               + jnp.dot(s.astype(BF16), vh, preferred_element_type=F32))
        qn = jnp.sum(qh.astype(F32) * n_ref[d, hd], axis=-1, keepdims=True)
        den = w_inter * qn + jnp.sum(s, axis=-1, keepdims=True)
        inv = 1.0 / jnp.maximum(jnp.abs(den), jnp.exp(-mj))
        h_ref[:, cols] = (num * inv).astype(BF16)

        dec_c = total - a_c + i_c
        dec_r = total - a_r + i_r
        m_new = jnp.maximum(total + m_prev, jnp.max(dec_r, axis=-1, keepdims=True))
        keep = jnp.exp(total + m_prev - m_new)
        wk_c = jnp.exp(dec_c - m_new) * scale
        wk_r = (jnp.exp(dec_r - m_new) * scale).astype(BF16)
        c_new = keep * c_ref[d, hd] + jnp.dot(kth * wk_r, vh, preferred_element_type=F32)
        c_ref[d, hd] = c_new
        cb_ref[d, hd] = c_new.astype(BF16)
        kw = k_ref[:, cols].astype(F32) * wk_c
        n_ref[d, hd] = keep * n_ref[d, hd] + jnp.sum(kw, axis=0, keepdims=True)
        m_ref[d, hd] = jnp.broadcast_to(m_new, m_ref.shape[2:])


def _ml_scan_kernel(qf_ref, kf_ref, ktf_ref, vf_ref, gcf_ref, grf_ref,
                    qb_ref, kb_ref, ktb_ref, vb_ref, gcb_ref, grb_ref,
                    hf_ref, hb_ref, c_ref, cb_ref, n_ref, m_ref, *, nch, tpc, ncp, ncs):
    cf = pl.program_id(0)
    cb = nch - 1 - cf

    def chunk_in_seq(ci):
        in_p = ci < tpc
        return (jnp.where(in_p, lax.rem(ci, ncp), lax.rem(ci - tpc, ncs)),
                jnp.where(in_p, ncp, ncs))

    pos_f, _ = chunk_in_seq(cf)
    pos_b, n_b = chunk_in_seq(cb)
    _ml_scan_dir(qf_ref, kf_ref, ktf_ref, vf_ref, gcf_ref, grf_ref, hf_ref, c_ref, cb_ref, n_ref,
                 m_ref, d=0, first=pos_f == 0)
    _ml_scan_dir(qb_ref, kb_ref, ktb_ref, vb_ref, gcb_ref, grb_ref, hb_ref, c_ref, cb_ref, n_ref,
                 m_ref, d=1, first=pos_b == n_b - 1)


def _ml_scan(q, k, kt, v, gt, tp, sp, ss):
    t = q.shape[0]
    L = ML_SCAN_L
    nch = t // L
    ng = 4 * ML_HEADS
    gt_rows = gt[:, :ng].T
    fwd = lambda i: (i, 0)
    bwd = lambda i: (nch - 1 - i, 0)
    fwd_t = lambda i: (0, i)
    bwd_t = lambda i: (0, nch - 1 - i)
    tok = lambda m: pl.BlockSpec((L, ML_E), m)
    tok_t = lambda m: pl.BlockSpec((ML_E, L), m)
    out = jax.ShapeDtypeStruct((t, ML_E), BF16)
    return pl.pallas_call(
        functools.partial(_ml_scan_kernel, nch=nch, tpc=tp // L, ncp=sp // L, ncs=ss // L),
        grid=(nch,),
        in_specs=[tok(fwd), tok(fwd), tok_t(fwd_t), tok(fwd), pl.BlockSpec((L, LANES), fwd),
                  pl.BlockSpec((ng, L), fwd_t),
                  tok(bwd), tok(bwd), tok_t(bwd_t), tok(bwd), pl.BlockSpec((L, LANES), bwd),
                  pl.BlockSpec((ng, L), bwd_t)],
        out_specs=[tok(fwd), tok(bwd)],
        out_shape=[out, out],
        scratch_shapes=[pltpu.VMEM((2, ML_HEADS, ML_DH, ML_DH), F32),
                        pltpu.VMEM((2, ML_HEADS, ML_DH, ML_DH), BF16),
                        pltpu.VMEM((2, ML_HEADS, 1, ML_DH), F32),
                        pltpu.VMEM((2, ML_HEADS, 1, LANES), F32)],
        compiler_params=_params(1),
        name="mlstm_scan",
    )(q, k, kt, v, gt, gt_rows, q, k, kt, v, gt, gt_rows)


def _ml_post_kernel(x_ref, mod_ref, hf_ref, hb_ref, xc_ref, z_ref, gn_ref, skip_ref, wd_ref,
                    g_ref, b_ref, o_ref, y_ref):
    for hd in range(ML_HEADS):
        cols = slice(hd * ML_DH, (hd + 1) * ML_DH)
        hs = hf_ref[:, cols].astype(F32) + hb_ref[:, cols].astype(F32)
        mu = jnp.mean(hs, axis=-1, keepdims=True)
        c = hs - mu
        var = jnp.mean(c * c, axis=-1, keepdims=True)
        hn = c * lax.rsqrt(var + NORM_EPS) * gn_ref[:, cols]
        y = (hn + skip_ref[:, cols] * xc_ref[:, cols].astype(F32)) * _silu(z_ref[:, cols].astype(F32))
        y_ref[:, cols] = y.astype(BF16)
    out = jnp.dot(y_ref[...], wd_ref[...], preferred_element_type=F32)
    o_ref[...] = _close(x_ref[...], out, mod_ref, 1, g_ref, b_ref)


def _ml_post(x, mod_l, hf, hb, xc, z, p, g, b, seg):
    t = x.shape[0]
    tm = min(ML_POST_TM, seg)
    gn = p["gn_g"].reshape(1, ML_E)
    skip = p["skip"].reshape(1, ML_E)
    w_down = p["w_down"].astype(BF16)
    tok = pl.BlockSpec((tm, ML_E), lambda i: (i, 0))
    return pl.pallas_call(
        _ml_post_kernel,
        grid=(t // tm,),
        in_specs=[
            pl.BlockSpec((tm, D_MODEL), lambda i: (i, 0)),
            pl.BlockSpec((None, 9, D_MODEL), lambda i: (i * tm // seg, 0, 0)),
            tok, tok, tok, tok,
            _resident(gn.shape), _resident(skip.shape), _resident(w_down.shape),
            _resident((1, D_MODEL)), _resident((1, D_MODEL)),
        ],
        out_specs=pl.BlockSpec((tm, D_MODEL), lambda i: (i, 0)),
        out_shape=jax.ShapeDtypeStruct((t, D_MODEL), F32),
        scratch_shapes=[pltpu.VMEM((tm, ML_E), BF16)],
        compiler_params=_params(1),
        name="mlstm_post",
    )(x, mod_l, hf, hb, xc, z, gn, skip, w_down, g, b)


def _rms(x, g_ref):
    return (x * lax.rsqrt(jnp.mean(x * x, axis=-1, keepdims=True) + NORM_EPS) * g_ref[...]).astype(BF16)


def _mla_pre_kernel(x_ref, mod_ref, cos_ref, sin_ref, wd_ref, qn_ref, kvn_ref, wq_ref, wkv_ref,
                    q_ref, k_ref, v_ref):
    scale = (MLA_NOPE + MLA_ROPE) ** -0.5 * LOG2_E
    hw = MLA_HEADS * LANES
    h = _modulated(x_ref[...], mod_ref, 1)
    c = jnp.dot(h, wd_ref[...], preferred_element_type=F32)
    cq = _rms(c[:, :MLA_Q_LORA], qn_ref)
    kv0 = MLA_Q_LORA + MLA_KV_LORA
    ckv = _rms(c[:, MLA_Q_LORA:kv0], kvn_ref)
    cos = cos_ref[...]
    sin = sin_ref[...]
    k_rope = (c[:, kv0:kv0 + LANES] * cos + c[:, kv0 + LANES:kv0 + 2 * LANES] * sin).astype(BF16)
    q = jnp.dot(cq, wq_ref[...], preferred_element_type=F32)
    kv = jnp.dot(ckv, wkv_ref[...], preferred_element_type=F32)
    v_ref[...] = kv[:, hw:].astype(BF16)
    for hd in range(MLA_HEADS):
        blk = slice(hd * LANES, (hd + 1) * LANES)
        rot = q[:, hw + hd * LANES:hw + (hd + 1) * LANES] * cos
        rot = rot + q[:, 2 * hw + hd * LANES:2 * hw + (hd + 1) * LANES] * sin
        o0 = hd * MLA_QK_W
        q_ref[:, o0:o0 + LANES] = (q[:, blk] * scale).astype(BF16)
        q_ref[:, o0 + LANES:o0 + 2 * LANES] = (rot * scale).astype(BF16)
        k_ref[:, o0:o0 + LANES] = kv[:, blk].astype(BF16)
        k_ref[:, o0 + LANES:o0 + 2 * LANES] = k_rope


def _rope_tables(smax):
    half = MLA_ROPE // 2
    inv = ROPE_THETA ** (-jnp.arange(0, MLA_ROPE, 2, dtype=F32) / MLA_ROPE)
    ang = jnp.arange(smax, dtype=F32)[:, None] * inv[None, :]
    z = jnp.zeros((smax, LANES - 2 * half), F32)
    cos = jnp.concatenate([jnp.cos(ang), jnp.cos(ang), z], axis=-1)
    sin = jnp.concatenate([jnp.sin(ang), jnp.sin(ang), z], axis=-1)
    return cos, sin


def _mla_pre(x, mod_l, p, seg, tp, sp, ss):
    t = x.shape[0]
    tm = min(MLA_PRE_TM, seg)
    half = MLA_ROPE // 2
    kv0 = MLA_Q_LORA + MLA_KV_LORA
    wd = p["w_down"]
    x1, x2 = wd[:, kv0:kv0 + half], wd[:, kv0 + half:]
    zpad = jnp.zeros((D_MODEL, LANES - 2 * half), F32)
    wd_ext = jnp.concatenate([wd[:, :kv0], x1, x2, zpad, -x2, x1, zpad], axis=-1).astype(BF16)
    wuq = p["w_uq"].reshape(MLA_Q_LORA, MLA_HEADS, MLA_NOPE + MLA_ROPE)
    q1, q2 = wuq[..., MLA_NOPE:MLA_NOPE + half], wuq[..., MLA_NOPE + half:]
    qz = jnp.zeros((MLA_Q_LORA, MLA_HEADS, LANES - 2 * half), F32)
    wq_all = jnp.concatenate([
        wuq[..., :MLA_NOPE].reshape(MLA_Q_LORA, -1),
        jnp.concatenate([q1, q2, qz], axis=-1).reshape(MLA_Q_LORA, -1),
        jnp.concatenate([-q2, q1, qz], axis=-1).reshape(MLA_Q_LORA, -1)], axis=-1).astype(BF16)
    wukv = p["w_ukv"].reshape(MLA_KV_LORA, MLA_HEADS, MLA_NOPE + MLA_V)
    wkv_all = jnp.concatenate([wukv[..., :MLA_NOPE].reshape(MLA_KV_LORA, -1),
                               wukv[..., MLA_NOPE:].reshape(MLA_KV_LORA, -1)], axis=-1).astype(BF16)
    qn = p["q_norm"].reshape(1, MLA_Q_LORA)
    kvn = p["kv_norm"].reshape(1, MLA_KV_LORA)
    cos, sin = _rope_tables(max(sp, ss))

    def pos_block(i):
        pos, _ = _seq_pos(i * tm, tp, sp, ss)
        return (pos // tm, 0)

    return pl.pallas_call(
        _mla_pre_kernel,
        grid=(t // tm,),
        in_specs=[
            pl.BlockSpec((tm, D_MODEL), lambda i: (i, 0)),
            pl.BlockSpec((None, 9, D_MODEL), lambda i: (i * tm // seg, 0, 0)),
            pl.BlockSpec((tm, LANES), pos_block),
            pl.BlockSpec((tm, LANES), pos_block),
            _resident(wd_ext.shape), _resident(qn.shape), _resident(kvn.shape),
            _resident(wq_all.shape), _resident(wkv_all.shape),
        ],
        out_specs=[pl.BlockSpec((tm, MLA_HEADS * MLA_QK_W), lambda i: (i, 0)),
                   pl.BlockSpec((tm, MLA_HEADS * MLA_QK_W), lambda i: (i, 0)),
                   pl.BlockSpec((tm, MLA_HEADS * MLA_V), lambda i: (i, 0))],
        out_shape=[jax.ShapeDtypeStruct((t, MLA_HEADS * MLA_QK_W), BF16),
                   jax.ShapeDtypeStruct((t, MLA_HEADS * MLA_QK_W), BF16),
                   jax.ShapeDtypeStruct((t, MLA_HEADS * MLA_V), BF16)],
        compiler_params=_params(1),
        name="mla_pre",
    )(x, mod_l, cos, sin, wd_ext, qn, kvn, wq_all, wkv_all)


def _attn_rows(q, k_ref, v_ref, kc):
    s_len = k_ref.shape[0]
    nt = (((1,), (1,)), ((), ()))
    m = l = acc = None
    for j in range(s_len // kc):
        rows = slice(j * kc, (j + 1) * kc)
        s = lax.dot_general(q, k_ref[rows, :], nt, preferred_element_type=F32)
        m_j = jnp.max(s, axis=-1, keepdims=True)
        if j == 0:
            m = m_j
            p = jnp.exp2(s - m)
            l = jnp.sum(p, axis=-1, keepdims=True)
            acc = jnp.dot(p.astype(BF16), v_ref[rows, :], preferred_element_type=F32)
        else:
            m_new = jnp.maximum(m, m_j)
            alpha = jnp.exp2(m - m_new)
            p = jnp.exp2(s - m_new)
            l = alpha * l + jnp.sum(p, axis=-1, keepdims=True)
            acc = alpha * acc + jnp.dot(p.astype(BF16), v_ref[rows, :], preferred_element_type=F32)
            m = m_new
    return acc * (1.0 / l)


def _attn_kernel(q_ref, k_ref, v_ref, o_ref, *, kc, n_chains):
    rc = q_ref.shape[0] // n_chains
    for r in range(n_chains):
        rows = slice(r * rc, (r + 1) * rc)
        o_ref[rows, :] = _attn_rows(q_ref[rows, :], k_ref, v_ref, kc).astype(BF16)


def _attention(q, k, v, row0, nb, s_len):
    tq = min(ATTN_TQ, s_len)
    n_chains = ATTN_ROW_CHAINS if tq % (ATTN_ROW_CHAINS * MXU_DIM) == 0 else 1
    kc = min(ATTN_KC, max(s_len // 2, MXU_DIM))
    nq = s_len // tq
    qb0 = row0 // tq
    sb0 = row0 // s_len
    return pl.pallas_call(
        functools.partial(_attn_kernel, kc=kc, n_chains=n_chains),
        grid=(nb, MLA_HEADS, nq),
        in_specs=[
            pl.BlockSpec((tq, MLA_QK_W), lambda b, h, i: (qb0 + b * nq + i, h)),
            pl.BlockSpec((s_len, MLA_QK_W), lambda b, h, i: (sb0 + b, h)),
            pl.BlockSpec((s_len, MLA_V), lambda b, h, i: (sb0 + b, h)),
        ],
        out_specs=pl.BlockSpec((tq, MLA_V), lambda b, h, i: (b * nq + i, h)),
        out_shape=jax.ShapeDtypeStruct((nb * s_len, MLA_HEADS * MLA_V), BF16),
        compiler_params=_params(3),
        name="mla_attention",
    )(q, k, v)


def _mla_post_kernel(x_ref, mod_ref, op_ref, os_ref, wo_ref, g_ref, b_ref, o_ref, *, n_first):
    o_in = jnp.where(pl.program_id(0) < n_first, op_ref[...], os_ref[...])
    y = jnp.dot(o_in, wo_ref[...], preferred_element_type=F32)
    o_ref[...] = _close(x_ref[...], y, mod_ref, 1, g_ref, b_ref)


def _mla_post(x, mod_l, o_first, o_second, w_o, g, b, seg):
    t = x.shape[0]
    tm = min(MLA_POST_TM, seg)
    w_o = w_o.astype(BF16)
    n_first, o_specs = _two_source_specs(o_first, o_second, tm, MLA_HEADS * MLA_V)
    return pl.pallas_call(
        functools.partial(_mla_post_kernel, n_first=n_first),
        grid=(t // tm,),
        in_specs=[
            pl.BlockSpec((tm, D_MODEL), lambda i: (i, 0)),
            pl.BlockSpec((None, 9, D_MODEL), lambda i: (i * tm // seg, 0, 0)),
            *o_specs,
            _resident(w_o.shape), _resident((1, D_MODEL)), _resident((1, D_MODEL)),
        ],
        out_specs=pl.BlockSpec((tm, D_MODEL), lambda i: (i, 0)),
        out_shape=jax.ShapeDtypeStruct((t, D_MODEL), F32),
        compiler_params=_params(1),
        name="mla_post",
    )(x, mod_l, o_first, o_second, w_o, g, b)


def kernel(x_prompt, x_sample, c_prompt, c_sample, ada_w, ada_b, ln_g, ln_b, ffn_w_in, ffn_w_out, gm_w_in, gm_b_in, gm_ln_g, gm_ln_b, gm_w_s, gm_b_s, gm_w_out, ml_w_up, ml_conv_w, ml_conv_b, ml_w_q, ml_w_k, ml_w_v, ml_w_gates, ml_b_gates, ml_gn_g, ml_skip, ml_w_down, mla_w_down, mla_q_norm, mla_kv_norm, mla_w_uq, mla_w_ukv, mla_w_o):
    bp, sp, d = x_prompt.shape
    bs, ss, _ = x_sample.shape
    tp = bp * sp
    ts = bs * ss
    seg = math.gcd(sp, ss)
    assert d == D_MODEL and seg % ML_SCAN_L == 0 and tp % ss == 0 and tp % sp == 0

    cond = jnp.concatenate([c_prompt, c_sample], axis=0)
    mod = _ada_table(cond, ada_w, ada_b)
    seg_seq = np.concatenate([np.repeat(np.arange(bp), sp // seg),
                              bp + np.repeat(np.arange(bs), ss // seg)])
    mod = mod[:, seg_seq].reshape(DEPTH, len(seg_seq), 9, d)
    ffn_w_in = ffn_w_in.astype(BF16)
    ffn_w_out = ffn_w_out.astype(BF16)

    x = (x_prompt.reshape(tp, d), x_sample.reshape(ts, d))
    counts = [0, 0, 0]
    for layer in range(DEPTH):
        mod_l = mod[layer]
        lg = lambda j: ln_g[layer, j].reshape(1, d)
        lb = lambda j: ln_b[layer, j].reshape(1, d)
        x = _ffn_sublayer(x, mod_l, 0, ffn_w_in[layer, 0], ffn_w_out[layer, 0], lg(0), lb(0), seg)
        kind = layer % N_MIXERS
        i = counts[kind]
        counts[kind] += 1
        if kind == 0:
            p = dict(w_in=gm_w_in[i], b_in=gm_b_in[i], ln_g=gm_ln_g[i], ln_b=gm_ln_b[i],
                     w_s=gm_w_s[i], b_s=gm_b_s[i], w_out=gm_w_out[i])
            x = _gmlp_sublayer(x, mod_l, p, lg(1), lb(1), seg)
        elif kind == 1:
            p = dict(w_up=ml_w_up[i], conv_w=ml_conv_w[i], conv_b=ml_conv_b[i], w_q=ml_w_q[i],
                     w_k=ml_w_k[i], w_v=ml_w_v[i], w_gates=ml_w_gates[i], b_gates=ml_b_gates[i],
                     gn_g=ml_gn_g[i], skip=ml_skip[i], w_down=ml_w_down[i])
            q, k, kt, v, xc, z, gt = _ml_pre(x, mod_l, p, seg, tp, sp, ss)
            hf, hb = _ml_scan(q, k, kt, v, gt, tp, sp, ss)
            x = _ml_post(x, mod_l, hf, hb, xc, z, p, lg(1), lb(1), seg)
        else:
            p = dict(w_down=mla_w_down[i], q_norm=mla_q_norm[i], kv_norm=mla_kv_norm[i],
                     w_uq=mla_w_uq[i], w_ukv=mla_w_ukv[i])
            q, k, v = _mla_pre(x, mod_l, p, seg, tp, sp, ss)
            o_p = _attention(q, k, v, 0, bp, sp)
            o_s = _attention(q, k, v, tp, bs, ss)
            x = _mla_post(x, mod_l, o_p, o_s, mla_w_o[i], lg(1), lb(1), seg)
        ffn2 = functools.partial(_ffn_sublayer, x, mod_l, 2, ffn_w_in[layer, 1],
                                 ffn_w_out[layer, 1], lg(2), lb(2), seg)
        if layer < DEPTH - 1:
            x = ffn2()
    return ffn2(row0=0, rows=tp).reshape(bp, sp, d), ffn2(row0=tp, rows=ts).reshape(bs, ss, d)
```
